```python
import jax, jax.numpy as jnp
from jax import lax
import numpy as np

D_MODEL = 1024
BATCH = 2
SEQ = 16384
DEPTH = 1
DEC_BATCH = 16
DEC_SEQ = 64
PAST_LEN = 4096

CHUNK = 64
MIX_WIDTH = D_MODEL
CONV_CH = MIX_WIDTH // 2
POOL_CH = MIX_WIDTH - CONV_CH
CONV_WIDTH = 31
CONV_HIST = CONV_WIDTH - 1
POOL_WINDOWS = (2, 4, 8, 16)
N_POOL_GROUPS = len(POOL_WINDOWS)
POOL_GROUP = POOL_CH // N_POOL_GROUPS
POOL_HIST = max(POOL_WINDOWS) - 1
IN_COLS = 2 * CONV_CH + POOL_CH
D_FF = ((8 * D_MODEL // 3 + 127) // 128) * 128
N_MOD = 9
ALPHA = (2.0 * DEPTH) ** 0.25
BETA = (8.0 * DEPTH) ** -0.25
LN_EPS = 1e-5

kernel_name = 'hybrid_conv_pool_streaming_encoder_step'


def _layernorm(x, g, b):
    xf = x.astype(jnp.float32)
    mu = jnp.mean(xf, axis=-1, keepdims=True)
    var = jnp.mean(jnp.square(xf - mu), axis=-1, keepdims=True)
    return ((xf - mu) * lax.rsqrt(var + LN_EPS)).astype(x.dtype) * g + b


def _swiglu(u, w_in, w_down):
    g, v = jnp.split(u @ w_in, 2, axis=-1)
    return (jax.nn.silu(g) * v) @ w_down


def _token_mixer(u, conv_hist, pool_hist, pos, w_in, b_in, conv_w, conv_b, cln_g, cln_b,
                 pool_w, pool_b, pool_scale, w_out, b_out):
    B, T, _ = u.shape
    z = u @ w_in + b_in
    za = z[..., :CONV_CH]
    zb = z[..., CONV_CH:2 * CONV_CH]
    zp = z[..., 2 * CONV_CH:]
    glu = za * jax.nn.sigmoid(zb)
    cin = jnp.concatenate([conv_hist.astype(glu.dtype), glu], axis=1)
    conv = lax.conv_general_dilated(cin, conv_w[:, None, :].astype(cin.dtype), window_strides=(1,),
                                    padding='VALID', dimension_numbers=('NWC', 'WIO', 'NWC'),
                                    feature_group_count=CONV_CH) + conv_b
    a = jax.nn.silu(_layernorm(conv, cln_g, cln_b))
    pin = jnp.concatenate([pool_hist.astype(zp.dtype), zp], axis=1)
    cs = jnp.pad(jnp.cumsum(pin.astype(jnp.float32), axis=1), ((0, 0), (1, 0), (0, 0)))
    end = cs[:, POOL_HIST + 1:]
    zp_f = zp.astype(jnp.float32)
    groups = []
    for g, w in enumerate(POOL_WINDOWS):
        sl = slice(g * POOL_GROUP, (g + 1) * POOL_GROUP)
        wsum = end[..., sl] - cs[:, POOL_HIST + 1 - w:POOL_HIST + 1 - w + T, sl]
        cnt = jnp.minimum(pos + 1, w).astype(jnp.float32)[None, :, None]
        groups.append(wsum / cnt - zp_f[..., sl])
    pooled = jnp.stack(groups, axis=2).astype(u.dtype)
    pm = jnp.einsum('btgc,gcd->btgd', pooled, pool_w) + pool_b
    pm = pm.reshape(B, T, POOL_CH) * pool_scale
    y = jnp.concatenate([a, pm], axis=-1) @ w_out + b_out
    return y, cin[:, -CONV_HIST:], pin[:, -POOL_HIST:]


def _trunk(x, c, conv_cache, pool_cache, pos0, ada_w, ada_b, ln_g, ln_b, ffn_w_in, ffn_w_down,
           mix_w_in, mix_b_in, conv_w, conv_b, conv_ln_g, conv_ln_b, pool_w, pool_b, pool_scale,
           mix_w_out, mix_b_out):
    B, T, _ = x.shape
    pos = pos0 + jnp.arange(T)
    new_conv, new_pool = [], []
    for l in range(DEPTH):
        mod = (jax.nn.silu(c) @ ada_w[l] + ada_b[l]).reshape(B, N_MOD, 1, D_MODEL)
        shift = lambda k: mod[:, 3 * k]
        scale = lambda k: mod[:, 3 * k + 1]
        gate = lambda k: mod[:, 3 * k + 2]
        u = x * (1 + scale(0)) + shift(0)
        x = _layernorm(ALPHA * x + 0.5 * gate(0) * _swiglu(u, ffn_w_in[l, 0], ffn_w_down[l, 0]),
                       ln_g[l, 0], ln_b[l, 0])
        u = x * (1 + scale(1)) + shift(1)
        m, hc, hp = _token_mixer(u, conv_cache[l], pool_cache[l], pos, mix_w_in[l], mix_b_in[l],
                                 conv_w[l], conv_b[l], conv_ln_g[l], conv_ln_b[l], pool_w[l],
                                 pool_b[l], pool_scale[l], mix_w_out[l], mix_b_out[l])
        x = _layernorm(ALPHA * x + gate(1) * m, ln_g[l, 1], ln_b[l, 1])
        u = x * (1 + scale(2)) + shift(2)
        x = _layernorm(ALPHA * x + 0.5 * gate(2) * _swiglu(u, ffn_w_in[l, 1], ffn_w_down[l, 1]),
                       ln_g[l, 2], ln_b[l, 2])
        new_conv.append(hc)
        new_pool.append(hp)
    return x, jnp.stack(new_conv, axis=0), jnp.stack(new_pool, axis=0)


def setup_inputs(seed: int = 0) -> dict:
    key = jax.random.key(seed)
    ks = jax.random.split(key, 24)
    f32 = jnp.float32
    def n(k, shape, s):
        return jax.random.normal(k, shape, f32) * s
    return {
        'x_prompt': n(ks[0], (BATCH, SEQ, D_MODEL), 1.0),
        'x_sample': n(ks[1], (DEC_BATCH, DEC_SEQ, D_MODEL), 1.0),
        'cache_conv': n(ks[2], (DEPTH, DEC_BATCH, CONV_HIST, CONV_CH), 0.5),
        'cache_pool': n(ks[3], (DEPTH, DEC_BATCH, POOL_HIST, POOL_CH), 1.0),
        'c_prompt': n(ks[4], (BATCH, D_MODEL), 1.0),
        'c_sample': n(ks[5], (DEC_BATCH, D_MODEL), 1.0),
        'ada_w': n(ks[6], (DEPTH, D_MODEL, N_MOD * D_MODEL), D_MODEL ** -0.5),
        'ada_b': n(ks[7], (DEPTH, N_MOD * D_MODEL), 0.02),
        'ln_g': 1.0 + n(ks[8], (DEPTH, 3, D_MODEL), 0.05),
        'ln_b': n(ks[9], (DEPTH, 3, D_MODEL), 0.02),
        'ffn_w_in': n(ks[10], (DEPTH, 2, D_MODEL, 2 * D_FF), D_MODEL ** -0.5),
        'ffn_w_down': n(ks[11], (DEPTH, 2, D_FF, D_MODEL), BETA * D_FF ** -0.5),
        'mix_w_in': n(ks[12], (DEPTH, D_MODEL, IN_COLS), D_MODEL ** -0.5),
        'mix_b_in': n(ks[13], (DEPTH, IN_COLS), 0.02),
        'conv_w': n(ks[14], (DEPTH, CONV_WIDTH, CONV_CH), CONV_WIDTH ** -0.5),
        'conv_b': n(ks[15], (DEPTH, CONV_CH), 0.02),
        'conv_ln_g': 1.0 + n(ks[16], (DEPTH, CONV_CH), 0.05),
        'conv_ln_b': n(ks[17], (DEPTH, CONV_CH), 0.02),
        'pool_w': n(ks[18], (DEPTH, N_POOL_GROUPS, POOL_GROUP, POOL_GROUP), POOL_GROUP ** -0.5),
        'pool_b': n(ks[19], (DEPTH, N_POOL_GROUPS, POOL_GROUP), 0.02),
        'pool_scale': 1.0 + n(ks[20], (DEPTH, POOL_CH), 0.1),
        'mix_w_out': n(ks[21], (DEPTH, MIX_WIDTH, D_MODEL), BETA * MIX_WIDTH ** -0.5),
        'mix_b_out': n(ks[22], (DEPTH, D_MODEL), 0.02),
    }


def reference(x_prompt, x_sample, cache_conv, cache_pool, c_prompt, c_sample, ada_w, ada_b, ln_g,
              ln_b, ffn_w_in, ffn_w_down, mix_w_in, mix_b_in, conv_w, conv_b, conv_ln_g, conv_ln_b,
              pool_w, pool_b, pool_scale, mix_w_out, mix_b_out):
    zero_conv = jnp.zeros((DEPTH, x_prompt.shape[0], CONV_HIST, CONV_CH), x_prompt.dtype)
    zero_pool = jnp.zeros((DEPTH, x_prompt.shape[0], POOL_HIST, POOL_CH), x_prompt.dtype)
    y_prompt, state_conv_prompt, state_pool_prompt = _trunk(
        x_prompt, c_prompt, zero_conv, zero_pool, 0, ada_w, ada_b, ln_g, ln_b, ffn_w_in, ffn_w_down,
        mix_w_in, mix_b_in, conv_w, conv_b, conv_ln_g, conv_ln_b, pool_w, pool_b, pool_scale,
        mix_w_out, mix_b_out)
    y_sample, state_conv_sample, state_pool_sample = _trunk(
        x_sample, c_sample, cache_conv, cache_pool, PAST_LEN, ada_w, ada_b, ln_g, ln_b, ffn_w_in,
        ffn_w_down, mix_w_in, mix_b_in, conv_w, conv_b, conv_ln_g, conv_ln_b, pool_w, pool_b,
        pool_scale, mix_w_out, mix_b_out)
    return (y_prompt, y_sample, state_conv_prompt, state_pool_prompt, state_conv_sample, state_pool_sample)
```

```python
import functools

import jax
import jax.numpy as jnp
from jax.experimental import pallas as pl
from jax.experimental.pallas import tpu as pltpu

D_MODEL = 1024
D_FF = 2816
CONV_CH = 512
POOL_CH = 512
CONV_WIDTH = 31
CONV_HIST = CONV_WIDTH - 1
POOL_WINDOWS = (2, 4, 8, 16)
POOL_GROUP = POOL_CH // len(POOL_WINDOWS)
POOL_HIST = max(POOL_WINDOWS) - 1
IN_COLS = 2 * CONV_CH + POOL_CH
N_MOD = 9
DEPTH = 1
ALPHA = (2.0 * DEPTH) ** 0.25
LN_EPS = 1e-5
PAST_LEN = 4096

SUBLANES = 8
CONV_OFF = 32
POOL_OFF = 16
VMEM_LIMIT_BYTES = 56 * 1024 * 1024

F32 = jnp.float32
BF16 = jnp.bfloat16


def _layernorm(y, g, b):
    mu = jnp.mean(y, axis=-1, keepdims=True)
    d = y - mu
    var = jnp.mean(d * d, axis=-1, keepdims=True)
    return d * jax.lax.rsqrt(var + LN_EPS) * g + b


def _sigmoid(x):
    return 1.0 / (1.0 + jnp.exp(-x))


def _const_spec(shape):
    nd = len(shape)
    return pl.BlockSpec(shape, lambda *_: (0,) * nd, pipeline_mode=pl.Buffered(1))


def _ada_kernel(c_ref, w_ref, b_ref, o_ref):
    c = c_ref[...]
    s = (c * _sigmoid(c)).astype(BF16)
    o_ref[...] = jnp.dot(s, w_ref[...].astype(BF16), preferred_element_type=F32) + b_ref[...]


def _ada_mod(c, w, b):
    rows = c.shape[0]
    n = w.shape[1]
    bn = D_MODEL
    return pl.pallas_call(
        _ada_kernel,
        grid=(n // bn,),
        in_specs=[
            pl.BlockSpec((rows, D_MODEL), lambda j: (0, 0)),
            pl.BlockSpec((D_MODEL, bn), lambda j: (0, j)),
            pl.BlockSpec((1, bn), lambda j: (0, j)),
        ],
        out_specs=pl.BlockSpec((rows, bn), lambda j: (0, j)),
        out_shape=jax.ShapeDtypeStruct((rows, n), F32),
        compiler_params=pltpu.CompilerParams(dimension_semantics=("arbitrary",)),
        name="ada_mod",
    )(c, w, b.reshape(1, n))


def _ffn_kernel(x_ref, sh_ref, sc_ref, gt_ref, win_ref, wdn_ref, lng_ref, lnb_ref, o_ref, *, fc):
    s, tm, _ = x_ref.shape
    x = x_ref[...]
    u = (x * (1.0 + sc_ref[...]) + sh_ref[...]).reshape(s * tm, D_MODEL).astype(BF16)
    acc = jnp.zeros((s * tm, D_MODEL), F32)
    for j in range(D_FF // fc):
        g = jnp.dot(u, win_ref[:, j * fc:(j + 1) * fc], preferred_element_type=F32)
        v = jnp.dot(u, win_ref[:, D_FF + j * fc:D_FF + (j + 1) * fc], preferred_element_type=F32)
        h = (g * _sigmoid(g) * v).astype(BF16)
        acc = acc + jnp.dot(h, wdn_ref[j * fc:(j + 1) * fc, :], preferred_element_type=F32)
    y = ALPHA * x + (0.5 * gt_ref[...]) * acc.reshape(s, tm, D_MODEL)
    o_ref[...] = _layernorm(y, lng_ref[...], lnb_ref[...])


def _ffn_step(x, shift, scale, gate, w_in, w_down, ln_g, ln_b, *, seqs, tm, fc):
    b, t, _ = x.shape
    xspec = pl.BlockSpec((seqs, tm, D_MODEL), lambda i, j: (i, j, 0))
    mspec = pl.BlockSpec((seqs, 1, D_MODEL), lambda i, j: (i, 0, 0))
    return pl.pallas_call(
        functools.partial(_ffn_kernel, fc=fc),
        grid=(b // seqs, t // tm),
        in_specs=[xspec, mspec, mspec, mspec,
                  _const_spec(w_in.shape), _const_spec(w_down.shape),
                  _const_spec((1, D_MODEL)), _const_spec((1, D_MODEL))],
        out_specs=xspec,
        out_shape=jax.ShapeDtypeStruct(x.shape, F32),
        compiler_params=pltpu.CompilerParams(
            dimension_semantics=("arbitrary", "arbitrary"), vmem_limit_bytes=VMEM_LIMIT_BYTES),
        name="ffn_step",
    )(x, shift, scale, gate, w_in, w_down, ln_g.reshape(1, D_MODEL), ln_b.reshape(1, D_MODEL))


def _mixer_kernel(x_ref, sh_ref, sc_ref, gt_ref, cc_ref, pc_ref, win_ref, bin_ref, cw_ref, cb_ref,
                  clg_ref, clb_ref, pw_ref, pb_ref, ps_ref, wout_ref, bout_ref, lng_ref, lnb_ref,
                  o_ref, sconv_ref, spool_ref, cbuf, pbuf, *, pos0):
    s, tm, _ = x_ref.shape
    t_idx = pl.program_id(1)

    @pl.when(t_idx == 0)
    def _():
        cbuf[:, CONV_OFF - CONV_HIST:CONV_OFF, :] = cc_ref[...]
        pbuf[:, POOL_OFF - POOL_HIST:POOL_OFF, :] = pc_ref[...]

    x = x_ref[...]
    u = (x * (1.0 + sc_ref[...]) + sh_ref[...]).reshape(s * tm, D_MODEL).astype(BF16)
    z = jnp.dot(u, win_ref[...], preferred_element_type=F32) + bin_ref[...]
    za = z[:, :CONV_CH]
    zb = z[:, CONV_CH:2 * CONV_CH]
    zp = z[:, 2 * CONV_CH:]

    glu = za * _sigmoid(zb)
    cbuf[:, CONV_OFF:CONV_OFF + tm, :] = glu.reshape(s, tm, CONV_CH)
    conv = jnp.zeros((s, tm, CONV_CH), F32) + cb_ref[...]
    for k in range(CONV_WIDTH):
        lo = CONV_OFF - CONV_HIST + k
        conv = conv + cbuf[:, lo:lo + tm, :] * cw_ref[k:k + 1, :]
    a = _layernorm(conv, clg_ref[...], clb_ref[...])
    a = (a * _sigmoid(a)).reshape(s * tm, CONV_CH)
    new_conv_hist = cbuf[:, tm + CONV_OFF - CONV_HIST:tm + CONV_OFF, :]
    sconv_ref[...] = new_conv_hist
    cbuf[:, CONV_OFF - CONV_HIST:CONV_OFF, :] = new_conv_hist

    zp3 = zp.reshape(s, tm, POOL_CH)
    pbuf[:, POOL_OFF:POOL_OFF + tm, :] = zp3
    pos = pos0 + t_idx * tm + jax.lax.broadcasted_iota(jnp.int32, (1, tm, 1), 1)
    pms = []
    for gi, w in enumerate(POOL_WINDOWS):
        c0 = gi * POOL_GROUP
        wsum = zp3[:, :, c0:c0 + POOL_GROUP]
        for i in range(1, w):
            wsum = wsum + pbuf[:, POOL_OFF - i:POOL_OFF - i + tm, c0:c0 + POOL_GROUP]
        cnt = jnp.minimum(pos + 1, w).astype(F32)
        pooled = wsum / cnt - zp3[:, :, c0:c0 + POOL_GROUP]
        pooled = pooled.reshape(s * tm, POOL_GROUP).astype(BF16)
        pm = jnp.dot(pooled, pw_ref[gi], preferred_element_type=F32) + pb_ref[gi:gi + 1, :]
        pms.append(pm)
    pm = jnp.concatenate(pms, axis=-1) * ps_ref[...]
    new_pool_hist = pbuf[:, tm + POOL_OFF - POOL_HIST:tm + POOL_OFF, :]
    spool_ref[...] = new_pool_hist
    pbuf[:, POOL_OFF - POOL_HIST:POOL_OFF, :] = new_pool_hist

    mixed = jnp.concatenate([a, pm], axis=-1).astype(BF16)
    m = jnp.dot(mixed, wout_ref[...], preferred_element_type=F32) + bout_ref[...]
    y = ALPHA * x + gt_ref[...] * m.reshape(s, tm, D_MODEL)
    o_ref[...] = _layernorm(y, lng_ref[...], lnb_ref[...])


def _mixer_step(x, shift, scale, gate, conv_cache, pool_cache, w_in, b_in, conv_w, conv_b, cln_g,
                cln_b, pool_w, pool_b, pool_scale, w_out, b_out, ln_g, ln_b, *, seqs, tm, pos0):
    b, t, _ = x.shape
    assert tm >= CONV_HIST and tm % SUBLANES == 0
    xspec = pl.BlockSpec((seqs, tm, D_MODEL), lambda i, j: (i, j, 0))
    mspec = pl.BlockSpec((seqs, 1, D_MODEL), lambda i, j: (i, 0, 0))
    cspec = pl.BlockSpec((seqs, CONV_HIST, CONV_CH), lambda i, j: (i, 0, 0))
    pspec = pl.BlockSpec((seqs, POOL_HIST, POOL_CH), lambda i, j: (i, 0, 0))
    row = lambda v: v.reshape(1, v.shape[-1])
    consts = [w_in, row(b_in), conv_w, row(conv_b), row(cln_g), row(cln_b), pool_w, pool_b,
              row(pool_scale), w_out, row(b_out), row(ln_g), row(ln_b)]
    return pl.pallas_call(
        functools.partial(_mixer_kernel, pos0=pos0),
        grid=(b // seqs, t // tm),
        in_specs=[xspec, mspec, mspec, mspec, cspec, pspec] + [_const_spec(c.shape) for c in consts],
        out_specs=[xspec, cspec, pspec],
        out_shape=[jax.ShapeDtypeStruct(x.shape, F32),
                   jax.ShapeDtypeStruct((b, CONV_HIST, CONV_CH), F32),
                   jax.ShapeDtypeStruct((b, POOL_HIST, POOL_CH), F32)],
        scratch_shapes=[pltpu.VMEM((seqs, CONV_OFF + tm, CONV_CH), F32),
                        pltpu.VMEM((seqs, POOL_OFF + tm, POOL_CH), F32)],
        compiler_params=pltpu.CompilerParams(
            dimension_semantics=("arbitrary", "arbitrary"), vmem_limit_bytes=VMEM_LIMIT_BYTES),
        name="mixer_step",
    )(x, shift, scale, gate, conv_cache, pool_cache, *consts)


def _trunk(x, mod, conv_cache, pool_cache, pos0, weights, *, seqs, tm, fc):
    (ln_g, ln_b, ffn_w_in, ffn_w_down, mix_w_in, mix_b_in, conv_w, conv_b, conv_ln_g, conv_ln_b,
     pool_w, pool_b, pool_scale, mix_w_out, mix_b_out) = weights
    b = x.shape[0]
    mod = mod.reshape(b, N_MOD, 1, D_MODEL)
    shift = lambda k: mod[:, 3 * k]
    scale = lambda k: mod[:, 3 * k + 1]
    gate = lambda k: mod[:, 3 * k + 2]
    l = 0
    x = _ffn_step(x, shift(0), scale(0), gate(0), ffn_w_in[l, 0], ffn_w_down[l, 0],
                  ln_g[l, 0], ln_b[l, 0], seqs=seqs, tm=tm, fc=fc)
    x, sconv, spool = _mixer_step(
        x, shift(1), scale(1), gate(1), conv_cache[l], pool_cache[l], mix_w_in[l], mix_b_in[l],
        conv_w[l], conv_b[l], conv_ln_g[l], conv_ln_b[l], pool_w[l], pool_b[l], pool_scale[l],
        mix_w_out[l], mix_b_out[l], ln_g[l, 1], ln_b[l, 1], seqs=seqs, tm=tm, pos0=pos0)
    x = _ffn_step(x, shift(2), scale(2), gate(2), ffn_w_in[l, 1], ffn_w_down[l, 1],
                  ln_g[l, 2], ln_b[l, 2], seqs=seqs, tm=tm, fc=fc)
    return x, sconv[None], spool[None]


def kernel(x_prompt, x_sample, cache_conv, cache_pool, c_prompt, c_sample, ada_w, ada_b, ln_g, ln_b,
           ffn_w_in, ffn_w_down, mix_w_in, mix_b_in, conv_w, conv_b, conv_ln_g, conv_ln_b, pool_w,
           pool_b, pool_scale, mix_w_out, mix_b_out):
    nb_p = x_prompt.shape[0]
    mod = _ada_mod(jnp.concatenate([c_prompt, c_sample], axis=0), ada_w[0], ada_b[0])
    weights = (ln_g, ln_b, ffn_w_in.astype(BF16), ffn_w_down.astype(BF16), mix_w_in.astype(BF16),
               mix_b_in, conv_w, conv_b, conv_ln_g, conv_ln_b, pool_w.astype(BF16), pool_b,
               pool_scale, mix_w_out.astype(BF16), mix_b_out)
    zero_conv = jnp.zeros((DEPTH, nb_p, CONV_HIST, CONV_CH), F32)
    zero_pool = jnp.zeros((DEPTH, nb_p, POOL_HIST, POOL_CH), F32)
    y_p, sc_p, sp_p = _trunk(x_prompt, mod[:nb_p], zero_conv, zero_pool, 0, weights,
                             seqs=1, tm=512, fc=256)
    y_s, sc_s, sp_s = _trunk(x_sample, mod[nb_p:], cache_conv, cache_pool, PAST_LEN, weights,
                             seqs=8, tm=x_sample.shape[1], fc=256)
    return (y_p, y_s, sc_p, sp_p, sc_s, sp_s)
```

```python
import functools

import jax
import jax.numpy as jnp
from jax.experimental import pallas as pl
from jax.experimental.pallas import tpu as pltpu

D_MODEL = 1024
D_FF = 2816
CONV_CH = 512
POOL_CH = 512
CONV_WIDTH = 31
CONV_HIST = CONV_WIDTH - 1
POOL_WINDOWS = (2, 4, 8, 16)
POOL_GROUP = POOL_CH // len(POOL_WINDOWS)
POOL_HIST = max(POOL_WINDOWS) - 1
IN_COLS = 2 * CONV_CH + POOL_CH
N_MOD = 9
DEPTH = 1
ALPHA = (2.0 * DEPTH) ** 0.25
LN_EPS = 1e-5
PAST_LEN = 4096

SUBLANES = 8
LANES = 128
N_SEG = SUBLANES
CONV_SLABS = CONV_CH // LANES
POOL_SLABS = POOL_CH // LANES
MIX_CHUNK_ROWS = 8 * SUBLANES
VMEM_LIMIT_BYTES = 56 * 1024 * 1024

F32 = jnp.float32
BF16 = jnp.bfloat16


def _layernorm(y, g, b):
    mu = jnp.mean(y, axis=-1, keepdims=True)
    d = y - mu
    var = jnp.mean(d * d, axis=-1, keepdims=True)
    return d * jax.lax.rsqrt(var + LN_EPS) * g + b


def _layernorm_slabs(y, g, b):
    n = y.shape[0] * y.shape[2]
    mu = jnp.sum(jnp.sum(y, axis=0), axis=-1, keepdims=True) / n
    d = y - mu
    var = jnp.sum(jnp.sum(d * d, axis=0), axis=-1, keepdims=True) / n
    return d * jax.lax.rsqrt(var + LN_EPS) * g + b


def _sigmoid(x):
    return 1.0 / (1.0 + jnp.exp(-x))


def _const_spec(shape):
    nd = len(shape)
    return pl.BlockSpec(shape, lambda *_: (0,) * nd, pipeline_mode=pl.Buffered(1))


def _ada_kernel(c_ref, w_ref, b_ref, o_ref):
    c = c_ref[...]
    s = (c * _sigmoid(c)).astype(BF16)
    o_ref[...] = jnp.dot(s, w_ref[...].astype(BF16), preferred_element_type=F32) + b_ref[...]


def _ada_mod(c, w, b):
    rows = c.shape[0]
    n = w.shape[1]
    bn = D_MODEL
    return pl.pallas_call(
        _ada_kernel,
        grid=(n // bn,),
        in_specs=[
            pl.BlockSpec((rows, D_MODEL), lambda j: (0, 0)),
            pl.BlockSpec((D_MODEL, bn), lambda j: (0, j)),
            pl.BlockSpec((1, bn), lambda j: (0, j)),
        ],
        out_specs=pl.BlockSpec((rows, bn), lambda j: (0, j)),
        out_shape=jax.ShapeDtypeStruct((rows, n), F32),
        compiler_params=pltpu.CompilerParams(dimension_semantics=("arbitrary",)),
        name="ada_mod",
    )(c, w, b.reshape(1, n))


def _ffn_kernel(x_ref, sh_ref, sc_ref, gt_ref, win_ref, wdn_ref, lng_ref, lnb_ref, o_ref, *, fc):
    s, tm, _ = x_ref.shape
    x = x_ref[...]
    u = (x * (1.0 + sc_ref[...]) + sh_ref[...]).reshape(s * tm, D_MODEL).astype(BF16)
    acc = jnp.zeros((s * tm, D_MODEL), F32)
    for j in range(D_FF // fc):
        g = jnp.dot(u, win_ref[:, j * fc:(j + 1) * fc], preferred_element_type=F32)
        v = jnp.dot(u, win_ref[:, D_FF + j * fc:D_FF + (j + 1) * fc], preferred_element_type=F32)
        h = (g * _sigmoid(g) * v).astype(BF16)
        acc = acc + jnp.dot(h, wdn_ref[j * fc:(j + 1) * fc, :], preferred_element_type=F32)
    y = ALPHA * x + (0.5 * gt_ref[...]) * acc.reshape(s, tm, D_MODEL)
    o_ref[...] = _layernorm(y, lng_ref[...], lnb_ref[...])


def _ffn_step(x, shift, scale, gate, w_in, w_down, ln_g, ln_b, *, seqs, tm, fc):
    b, t, _ = x.shape
    xspec = pl.BlockSpec((seqs, tm, D_MODEL), lambda i, j: (i, j, 0))
    mspec = pl.BlockSpec((seqs, 1, D_MODEL), lambda i, j: (i, 0, 0))
    return pl.pallas_call(
        functools.partial(_ffn_kernel, fc=fc),
        grid=(b // seqs, t // tm),
        in_specs=[xspec, mspec, mspec, mspec,
                  _const_spec(w_in.shape), _const_spec(w_down.shape),
                  _const_spec((1, D_MODEL)), _const_spec((1, D_MODEL))],
        out_specs=xspec,
        out_shape=jax.ShapeDtypeStruct(x.shape, F32),
        compiler_params=pltpu.CompilerParams(
            dimension_semantics=("arbitrary", "arbitrary"), vmem_limit_bytes=VMEM_LIMIT_BYTES),
        name="ffn_step",
    )(x, shift, scale, gate, w_in, w_down, ln_g.reshape(1, D_MODEL), ln_b.reshape(1, D_MODEL))


def _seg_rows(start, count):
    return pl.ds(start, count, stride=N_SEG)


def _fill_history(buf, hist, seg_len, cache_ref, carry, state_ref, chained, t_idx):
    tail = SUBLANES * seg_len
    for c in range(buf.shape[0]):
        lanes = slice(c * LANES, (c + 1) * LANES)
        if chained:
            buf[c, _seg_rows(0, hist), :] = carry[:, lanes]
            for s in range(1, N_SEG):
                buf[c, _seg_rows(s, hist), :] = buf[c, _seg_rows(tail + s - 1, hist), :]
            new = buf[c, _seg_rows(tail + N_SEG - 1, hist), :]
            carry[:, lanes] = new
            state_ref[0, :, lanes] = new
        else:
            for s in range(N_SEG):
                buf[c, _seg_rows(s, hist), :] = cache_ref[s, :, lanes]
                state_ref[s, :, lanes] = buf[c, _seg_rows(tail + s, hist), :]


def _mixer_kernel(x_ref, sh_ref, sc_ref, gt_ref, cc_ref, pc_ref, win_ref, bin_ref, cw_ref, cb_ref,
                  clg_ref, clb_ref, pw_ref, pb_ref, ps_ref, wout_ref, bout_ref, lng_ref, lnb_ref,
                  o_ref, sconv_ref, spool_ref, gbuf, pbuf, abuf, ccarry, pcarry, *, pos0, chained):
    n_seq, tt, _ = x_ref.shape
    rows = n_seq * tt
    seg_len = rows // N_SEG
    seg_span = SUBLANES * seg_len
    t_idx = pl.program_id(1)

    if chained:
        @pl.when(t_idx == 0)
        def _():
            ccarry[...] = cc_ref[0]
            pcarry[...] = pc_ref[0]

    x = x_ref[...]
    u = (x * (1.0 + sc_ref[...]) + sh_ref[...]).reshape(rows, D_MODEL).astype(BF16)
    z = jnp.dot(u, win_ref[...], preferred_element_type=F32) + bin_ref[...]
    glu = z[:, :CONV_CH] * _sigmoid(z[:, CONV_CH:2 * CONV_CH])
    zp = z[:, 2 * CONV_CH:]

    for s in range(N_SEG):
        seg = slice(s * seg_len, (s + 1) * seg_len)
        for c in range(CONV_SLABS):
            gbuf[c, _seg_rows(SUBLANES * CONV_HIST + s, seg_len), :] = glu[seg, c * LANES:(c + 1) * LANES]
        for c in range(POOL_SLABS):
            pbuf[c, _seg_rows(SUBLANES * POOL_HIST + s, seg_len), :] = zp[seg, c * LANES:(c + 1) * LANES]
    _fill_history(gbuf, CONV_HIST, seg_len, cc_ref, ccarry, sconv_ref, chained, t_idx)
    _fill_history(pbuf, POOL_HIST, seg_len, pc_ref, pcarry, spool_ref, chained, t_idx)

    r = jax.lax.broadcasted_iota(jnp.int32, (MIX_CHUNK_ROWS, 1), 0)
    if chained:
        pos_base = pos0 + t_idx * rows + (r % SUBLANES) * seg_len + r // SUBLANES
    else:
        pos_base = pos0 + t_idx * seg_len + r // SUBLANES
    for q in range(seg_span // MIX_CHUNK_ROWS):
        r0 = q * MIX_CHUNK_ROWS
        conv = cb_ref[...] + gbuf[:, r0:r0 + MIX_CHUNK_ROWS, :] * cw_ref[0]
        for k in range(1, CONV_WIDTH):
            lo = r0 + SUBLANES * k
            conv = conv + gbuf[:, lo:lo + MIX_CHUNK_ROWS, :] * cw_ref[k]
        a = _layernorm_slabs(conv, clg_ref[...], clb_ref[...])
        abuf[0:CONV_SLABS, r0:r0 + MIX_CHUNK_ROWS, :] = a * _sigmoid(a)
        pos = pos_base + r0 // SUBLANES
        for gi, w in enumerate(POOL_WINDOWS):
            lo = r0 + SUBLANES * POOL_HIST
            cur = pbuf[gi, lo:lo + MIX_CHUNK_ROWS, :]
            wsum = cur
            for i in range(1, w):
                wsum = wsum + pbuf[gi, lo - SUBLANES * i:lo - SUBLANES * i + MIX_CHUNK_ROWS, :]
            inv_cnt = 1.0 / jnp.minimum(pos + 1, w).astype(F32)
            abuf[CONV_SLABS + gi, r0:r0 + MIX_CHUNK_ROWS, :] = wsum * inv_cnt - cur

    for gi in range(len(POOL_WINDOWS)):
        pooled = abuf[CONV_SLABS + gi].astype(BF16)
        pm = jnp.dot(pooled, pw_ref[gi], preferred_element_type=F32) + pb_ref[gi]
        abuf[CONV_SLABS + gi] = pm * ps_ref[gi]

    mixed = jnp.concatenate(
        [jnp.concatenate([abuf[c, _seg_rows(s, seg_len), :] for c in range(abuf.shape[0])], axis=-1)
         for s in range(N_SEG)], axis=0).astype(BF16)
    m = jnp.dot(mixed, wout_ref[...], preferred_element_type=F32) + bout_ref[...]
    y = ALPHA * x + gt_ref[...] * m.reshape(n_seq, tt, D_MODEL)
    o_ref[...] = _layernorm(y, lng_ref[...], lnb_ref[...])


def _mixer_step(x, shift, scale, gate, conv_cache, pool_cache, w_in, b_in, conv_w, conv_b, cln_g,
                cln_b, pool_w, pool_b, pool_scale, w_out, b_out, ln_g, ln_b, *, seqs, tm, pos0):
    b, t, _ = x.shape
    chained = seqs == 1
    seg_len = tm // N_SEG if chained else tm
    assert seqs in (1, N_SEG) and (chained or tm == t)
    assert seg_len >= CONV_HIST and seg_len % SUBLANES == 0
    xspec = pl.BlockSpec((seqs, tm, D_MODEL), lambda i, j: (i, j, 0))
    mspec = pl.BlockSpec((seqs, 1, D_MODEL), lambda i, j: (i, 0, 0))
    cspec = pl.BlockSpec((seqs, CONV_HIST, CONV_CH), lambda i, j: (i, 0, 0))
    pspec = pl.BlockSpec((seqs, POOL_HIST, POOL_CH), lambda i, j: (i, 0, 0))
    row = lambda v: v.reshape(1, v.shape[-1])
    slabs = lambda v: v.reshape(v.shape[:-1] + (v.shape[-1] // LANES, 1, LANES))
    consts = [w_in, row(b_in), slabs(conv_w), slabs(conv_b), slabs(cln_g), slabs(cln_b), pool_w,
              pool_b.reshape(len(POOL_WINDOWS), 1, POOL_GROUP), slabs(pool_scale), w_out, row(b_out),
              row(ln_g), row(ln_b)]
    return pl.pallas_call(
        functools.partial(_mixer_kernel, pos0=pos0, chained=chained),
        grid=(b // seqs, t // tm),
        in_specs=[xspec, mspec, mspec, mspec, cspec, pspec] + [_const_spec(c.shape) for c in consts],
        out_specs=[xspec, cspec, pspec],
        out_shape=[jax.ShapeDtypeStruct(x.shape, F32),
                   jax.ShapeDtypeStruct((b, CONV_HIST, CONV_CH), F32),
                   jax.ShapeDtypeStruct((b, POOL_HIST, POOL_CH), F32)],
        scratch_shapes=[pltpu.VMEM((CONV_SLABS, SUBLANES * (CONV_HIST + seg_len), LANES), F32),
                        pltpu.VMEM((POOL_SLABS, SUBLANES * (POOL_HIST + seg_len), LANES), F32),
                        pltpu.VMEM((CONV_SLABS + POOL_SLABS, SUBLANES * seg_len, LANES), F32),
                        pltpu.VMEM((CONV_HIST, CONV_CH), F32),
                        pltpu.VMEM((POOL_HIST, POOL_CH), F32)],
        compiler_params=pltpu.CompilerParams(
            dimension_semantics=("arbitrary", "arbitrary"), vmem_limit_bytes=VMEM_LIMIT_BYTES),
        name="mixer_step",
    )(x, shift, scale, gate, conv_cache, pool_cache, *consts)


def _trunk(x, mod, conv_cache, pool_cache, pos0, weights, *, seqs, tm, fc):
    (ln_g, ln_b, ffn_w_in, ffn_w_down, mix_w_in, mix_b_in, conv_w, conv_b, conv_ln_g, conv_ln_b,
     pool_w, pool_b, pool_scale, mix_w_out, mix_b_out) = weights
    b = x.shape[0]
    mod = mod.reshape(b, N_MOD, 1, D_MODEL)
    shift = lambda k: mod[:, 3 * k]
    scale = lambda k: mod[:, 3 * k + 1]
    gate = lambda k: mod[:, 3 * k + 2]
    l = 0
    x = _ffn_step(x, shift(0), scale(0), gate(0), ffn_w_in[l, 0], ffn_w_down[l, 0],
                  ln_g[l, 0], ln_b[l, 0], seqs=seqs, tm=tm, fc=fc)
    x, sconv, spool = _mixer_step(
        x, shift(1), scale(1), gate(1), conv_cache[l], pool_cache[l], mix_w_in[l], mix_b_in[l],
        conv_w[l], conv_b[l], conv_ln_g[l], conv_ln_b[l], pool_w[l], pool_b[l], pool_scale[l],
        mix_w_out[l], mix_b_out[l], ln_g[l, 1], ln_b[l, 1], seqs=seqs, tm=tm, pos0=pos0)
    x = _ffn_step(x, shift(2), scale(2), gate(2), ffn_w_in[l, 1], ffn_w_down[l, 1],
                  ln_g[l, 2], ln_b[l, 2], seqs=seqs, tm=tm, fc=fc)
    return x, sconv[None], spool[None]


def kernel(x_prompt, x_sample, cache_conv, cache_pool, c_prompt, c_sample, ada_w, ada_b, ln_g, ln_b,
           ffn_w_in, ffn_w_down, mix_w_in, mix_b_in, conv_w, conv_b, conv_ln_g, conv_ln_b, pool_w,
           pool_b, pool_scale, mix_w_out, mix_b_out):
    nb_p = x_prompt.shape[0]
    mod = _ada_mod(jnp.concatenate([c_prompt, c_sample], axis=0), ada_w[0], ada_b[0])
    weights = (ln_g, ln_b, ffn_w_in.astype(BF16), ffn_w_down.astype(BF16), mix_w_in.astype(BF16),
               mix_b_in, conv_w, conv_b, conv_ln_g, conv_ln_b, pool_w.astype(BF16), pool_b,
               pool_scale, mix_w_out.astype(BF16), mix_b_out)
    zero_conv = jnp.zeros((DEPTH, nb_p, CONV_HIST, CONV_CH), F32)
    zero_pool = jnp.zeros((DEPTH, nb_p, POOL_HIST, POOL_CH), F32)
    y_p, sc_p, sp_p = _trunk(x_prompt, mod[:nb_p], zero_conv, zero_pool, 0, weights,
                             seqs=1, tm=512, fc=256)
    y_s, sc_s, sp_s = _trunk(x_sample, mod[nb_p:], cache_conv, cache_pool, PAST_LEN, weights,
                             seqs=N_SEG, tm=x_sample.shape[1], fc=256)
    return (y_p, y_s, sc_p, sp_p, sc_s, sp_s)
```

```python
import functools

import jax
import jax.numpy as jnp
from jax.experimental import pallas as pl
from jax.experimental.pallas import tpu as pltpu

D_MODEL = 1024
D_FF = 2816
CONV_CH = 512
POOL_CH = 512
CONV_WIDTH = 31
CONV_HIST = CONV_WIDTH - 1
POOL_WINDOWS = (2, 4, 8, 16)
POOL_GROUP = POOL_CH // len(POOL_WINDOWS)
POOL_HIST = max(POOL_WINDOWS) - 1
IN_COLS = 2 * CONV_CH + POOL_CH
N_MOD = 9
DEPTH = 1
ALPHA = (2.0 * DEPTH) ** 0.25
LN_EPS = 1e-5
PAST_LEN = 4096

SUBLANES = 8
LANES = 128
N_SEG = SUBLANES
CONV_SLABS = CONV_CH // LANES
POOL_SLABS = POOL_CH // LANES
MIX_CHUNK_ROWS = 8 * SUBLANES
FFN_CHUNK = 256
FFN_TILE = 512
MIXER_TILE = 512
VMEM_LIMIT_BYTES = 56 * 1024 * 1024

F32 = jnp.float32
BF16 = jnp.bfloat16


def _layernorm(y, g, b):
    mu = jnp.mean(y, axis=-1, keepdims=True)
    d = y - mu
    var = jnp.mean(d * d, axis=-1, keepdims=True)
    return d * jax.lax.rsqrt(var + LN_EPS) * g + b


def _layernorm_slabs(y, g, b):
    n = y.shape[0] * y.shape[2]
    mu = jnp.sum(jnp.sum(y, axis=0), axis=-1, keepdims=True) / n
    d = y - mu
    var = jnp.sum(jnp.sum(d * d, axis=0), axis=-1, keepdims=True) / n
    return d * jax.lax.rsqrt(var + LN_EPS) * g + b


def _sigmoid(x):
    return 1.0 / (1.0 + jnp.exp(-x))


def _const_spec(shape):
    nd = len(shape)
    return pl.BlockSpec(shape, lambda *_: (0,) * nd, pipeline_mode=pl.Buffered(1))


def _stack_spec(shape, *lead):
    nd = len(shape) - len(lead)
    return pl.BlockSpec((None,) * len(lead) + tuple(shape[len(lead):]),
                        lambda *_: tuple(lead) + (0,) * nd, pipeline_mode=pl.Buffered(1))


def _mod_specs(seqs, row_block0, k):
    return [pl.BlockSpec((seqs, None, 1, D_MODEL),
                         lambda i, j, m=m: (row_block0 + i, 3 * k + m, 0, 0)) for m in range(3)]


def _ada_kernel(c_ref, w_ref, b_ref, o_ref):
    c = c_ref[...]
    s = (c * _sigmoid(c)).astype(BF16)
    o_ref[...] = jnp.dot(s, w_ref[...].astype(BF16), preferred_element_type=F32) + b_ref[...]


def _ada_mod(c, w, b):
    rows = c.shape[0]
    n = w.shape[1]
    bn = D_MODEL
    return pl.pallas_call(
        _ada_kernel,
        grid=(n // bn,),
        in_specs=[
            pl.BlockSpec((rows, D_MODEL), lambda j: (0, 0)),
            pl.BlockSpec((D_MODEL, bn), lambda j: (0, j)),
            pl.BlockSpec((1, bn), lambda j: (0, j)),
        ],
        out_specs=pl.BlockSpec((rows, bn), lambda j: (0, j)),
        out_shape=jax.ShapeDtypeStruct((rows, n), F32),
        compiler_params=pltpu.CompilerParams(dimension_semantics=("arbitrary",)),
        name="ada_mod",
    )(c, w, b.reshape(1, n))


def _ffn_kernel(x_ref, sh_ref, sc_ref, gt_ref, win_ref, wdn_ref, lng_ref, lnb_ref, o_ref, *, fc):
    s, tm, _ = x_ref.shape
    x = x_ref[...]
    u = (x * (1.0 + sc_ref[...]) + sh_ref[...]).reshape(s * tm, D_MODEL).astype(BF16)
    acc = jnp.zeros((s * tm, D_MODEL), F32)
    for j in range(D_FF // fc):
        g = jnp.dot(u, win_ref[:, j * fc:(j + 1) * fc], preferred_element_type=F32)
        v = jnp.dot(u, win_ref[:, D_FF + j * fc:D_FF + (j + 1) * fc], preferred_element_type=F32)
        h = (g * _sigmoid(g) * v).astype(BF16)
        acc = acc + jnp.dot(h, wdn_ref[j * fc:(j + 1) * fc, :], preferred_element_type=F32)
    y = ALPHA * x + (0.5 * gt_ref[...]) * acc.reshape(s, tm, D_MODEL)
    o_ref[...] = _layernorm(y, lng_ref[...], lnb_ref[...])


def _ffn_step(x, mod, row_block0, k, w_in, w_down, ln_g, ln_b, layer, half, *, seqs, tm, fc):
    b, t, _ = x.shape
    xspec = pl.BlockSpec((seqs, tm, D_MODEL), lambda i, j: (i, j, 0))
    return pl.pallas_call(
        functools.partial(_ffn_kernel, fc=fc),
        grid=(b // seqs, t // tm),
        in_specs=[xspec] + _mod_specs(seqs, row_block0, k) + [
            _stack_spec(w_in.shape, layer, half), _stack_spec(w_down.shape, layer, half),
            _stack_spec(ln_g.shape, layer, k), _stack_spec(ln_b.shape, layer, k)],
        out_specs=xspec,
        out_shape=jax.ShapeDtypeStruct(x.shape, F32),
        compiler_params=pltpu.CompilerParams(
            dimension_semantics=("arbitrary", "arbitrary"), vmem_limit_bytes=VMEM_LIMIT_BYTES),
        name="ffn_step",
    )(x, mod, mod, mod, w_in, w_down, ln_g, ln_b)


def _seg_rows(start, count):
    return pl.ds(start, count, stride=N_SEG)


def _fill_history(buf, hist, seg_len, cache_ref, carry, state_ref, chained):
    tail = SUBLANES * seg_len
    for c in range(buf.shape[0]):
        lanes = slice(c * LANES, (c + 1) * LANES)
        if chained:
            buf[c, _seg_rows(0, hist), :] = carry[:, lanes]
            for s in range(1, N_SEG):
                buf[c, _seg_rows(s, hist), :] = buf[c, _seg_rows(tail + s - 1, hist), :]
            new = buf[c, _seg_rows(tail + N_SEG - 1, hist), :]
            carry[:, lanes] = new
            state_ref[0, :, lanes] = new
        else:
            for s in range(N_SEG):
                buf[c, _seg_rows(s, hist), :] = cache_ref[s, :, lanes]
                state_ref[s, :, lanes] = buf[c, _seg_rows(tail + s, hist), :]


def _mixer_kernel(x_ref, sh_ref, sc_ref, gt_ref, cc_ref, pc_ref, win_ref, bin_ref, cw_ref, cb_ref,
                  clg_ref, clb_ref, pw_ref, pb_ref, ps_ref, wout_ref, bout_ref, lng_ref, lnb_ref,
                  o_ref, sconv_ref, spool_ref, gbuf, pbuf, abuf, ccarry, pcarry, *, pos0, chained):
    n_seq, tt, _ = x_ref.shape
    rows = n_seq * tt
    seg_len = rows // N_SEG
    seg_span = SUBLANES * seg_len
    t_idx = pl.program_id(1)

    if chained:
        @pl.when(t_idx == 0)
        def _():
            ccarry[...] = cc_ref[0]
            pcarry[...] = pc_ref[0]

    x = x_ref[...]
    u = (x * (1.0 + sc_ref[...]) + sh_ref[...]).reshape(rows, D_MODEL).astype(BF16)
    z = jnp.dot(u, win_ref[...], preferred_element_type=F32) + bin_ref[...]
    glu = z[:, :CONV_CH] * _sigmoid(z[:, CONV_CH:2 * CONV_CH])
    zp = z[:, 2 * CONV_CH:]

    for s in range(N_SEG):
        seg = slice(s * seg_len, (s + 1) * seg_len)
        for c in range(CONV_SLABS):
            gbuf[c, _seg_rows(SUBLANES * CONV_HIST + s, seg_len), :] = glu[seg, c * LANES:(c + 1) * LANES]
        for c in range(POOL_SLABS):
            pbuf[c, _seg_rows(SUBLANES * POOL_HIST + s, seg_len), :] = zp[seg, c * LANES:(c + 1) * LANES]
    _fill_history(gbuf, CONV_HIST, seg_len, cc_ref, ccarry, sconv_ref, chained)
    _fill_history(pbuf, POOL_HIST, seg_len, pc_ref, pcarry, spool_ref, chained)

    r = jax.lax.broadcasted_iota(jnp.int32, (MIX_CHUNK_ROWS, 1), 0)
    if chained:
        pos_base = pos0 + t_idx * rows + (r % SUBLANES) * seg_len + r // SUBLANES
    else:
        pos_base = pos0 + t_idx * seg_len + r // SUBLANES
    for q in range(seg_span // MIX_CHUNK_ROWS):
        r0 = q * MIX_CHUNK_ROWS
        conv = cb_ref[...] + gbuf[:, r0:r0 + MIX_CHUNK_ROWS, :] * cw_ref[0]
        for k in range(1, CONV_WIDTH):
            lo = r0 + SUBLANES * k
            conv = conv + gbuf[:, lo:lo + MIX_CHUNK_ROWS, :] * cw_ref[k]
        a = _layernorm_slabs(conv, clg_ref[...], clb_ref[...])
        abuf[0:CONV_SLABS, r0:r0 + MIX_CHUNK_ROWS, :] = a * _sigmoid(a)
        pos = pos_base + r0 // SUBLANES
        for gi, w in enumerate(POOL_WINDOWS):
            lo = r0 + SUBLANES * POOL_HIST
            cur = pbuf[gi, lo:lo + MIX_CHUNK_ROWS, :]
            wsum = cur
            for i in range(1, w):
                wsum = wsum + pbuf[gi, lo - SUBLANES * i:lo - SUBLANES * i + MIX_CHUNK_ROWS, :]
            inv_cnt = 1.0 / jnp.minimum(pos + 1, w).astype(F32)
            abuf[CONV_SLABS + gi, r0:r0 + MIX_CHUNK_ROWS, :] = wsum * inv_cnt - cur

    for gi in range(len(POOL_WINDOWS)):
        pooled = abuf[CONV_SLABS + gi].astype(BF16)
        pm = jnp.dot(pooled, pw_ref[gi], preferred_element_type=F32) + pb_ref[gi]
        abuf[CONV_SLABS + gi] = pm * ps_ref[gi]

    mixed = jnp.concatenate(
        [jnp.concatenate([abuf[c, _seg_rows(s, seg_len), :] for c in range(abuf.shape[0])], axis=-1)
         for s in range(N_SEG)], axis=0).astype(BF16)
    m = jnp.dot(mixed, wout_ref[...], preferred_element_type=F32) + bout_ref[...]
    y = ALPHA * x + gt_ref[...] * m.reshape(n_seq, tt, D_MODEL)
    o_ref[...] = _layernorm(y, lng_ref[...], lnb_ref[...])


def _mixer_step(x, mod, row_block0, conv_cache, pool_cache, w_in, b_in, conv_w, conv_b, cln_g,
                cln_b, pool_w, pool_b, pool_scale, w_out, b_out, ln_g, ln_b, *, seqs, tm, pos0):
    b, t, _ = x.shape
    chained = seqs == 1
    seg_len = tm // N_SEG if chained else tm
    assert seqs in (1, N_SEG) and (chained or tm == t)
    assert seg_len >= CONV_HIST and seg_len % SUBLANES == 0
    xspec = pl.BlockSpec((seqs, tm, D_MODEL), lambda i, j: (i, j, 0))
    cspec = pl.BlockSpec((seqs, CONV_HIST, CONV_CH), lambda i, j: (i, 0, 0))
    pspec = pl.BlockSpec((seqs, POOL_HIST, POOL_CH), lambda i, j: (i, 0, 0))
    row = lambda v: v.reshape(1, v.shape[-1])
    slabs = lambda v: v.reshape(v.shape[:-1] + (v.shape[-1] // LANES, 1, LANES))
    consts = [w_in, row(b_in), slabs(conv_w), slabs(conv_b), slabs(cln_g), slabs(cln_b), pool_w,
              pool_b.reshape(len(POOL_WINDOWS), 1, POOL_GROUP), slabs(pool_scale), w_out, row(b_out),
              row(ln_g), row(ln_b)]
    return pl.pallas_call(
        functools.partial(_mixer_kernel, pos0=pos0, chained=chained),
        grid=(b // seqs, t // tm),
        in_specs=([xspec] + _mod_specs(seqs, row_block0, 1) + [cspec, pspec]
                  + [_const_spec(c.shape) for c in consts]),
        out_specs=[xspec, cspec, pspec],
        out_shape=[jax.ShapeDtypeStruct(x.shape, F32),
                   jax.ShapeDtypeStruct((b, CONV_HIST, CONV_CH), F32),
                   jax.ShapeDtypeStruct((b, POOL_HIST, POOL_CH), F32)],
        scratch_shapes=[pltpu.VMEM((CONV_SLABS, SUBLANES * (CONV_HIST + seg_len), LANES), F32),
                        pltpu.VMEM((POOL_SLABS, SUBLANES * (POOL_HIST + seg_len), LANES), F32),
                        pltpu.VMEM((CONV_SLABS + POOL_SLABS, SUBLANES * seg_len, LANES), F32),
                        pltpu.VMEM((CONV_HIST, CONV_CH), F32),
                        pltpu.VMEM((POOL_HIST, POOL_CH), F32)],
        compiler_params=pltpu.CompilerParams(
            dimension_semantics=("arbitrary", "arbitrary"), vmem_limit_bytes=VMEM_LIMIT_BYTES),
        name="mixer_step",
    )(x, mod, mod, mod, conv_cache, pool_cache, *consts)


def _trunk(x, mod, row0, conv_cache, pool_cache, pos0, weights, *, seqs, tm, ffn_seqs, ffn_tm):
    (ln_g, ln_b, ffn_w_in, ffn_w_down, mix_w_in, mix_b_in, conv_w, conv_b, conv_ln_g, conv_ln_b,
     pool_w, pool_b, pool_scale, mix_w_out, mix_b_out) = weights
    l = 0
    ffn = functools.partial(_ffn_step, seqs=ffn_seqs, tm=ffn_tm, fc=FFN_CHUNK)
    x = ffn(x, mod, row0 // ffn_seqs, 0, ffn_w_in, ffn_w_down, ln_g, ln_b, l, 0)
    x, sconv, spool = _mixer_step(
        x, mod, row0 // seqs, conv_cache[l], pool_cache[l], mix_w_in[l], mix_b_in[l],
        conv_w[l], conv_b[l], conv_ln_g[l], conv_ln_b[l], pool_w[l], pool_b[l], pool_scale[l],
        mix_w_out[l], mix_b_out[l], ln_g[l, 1, 0], ln_b[l, 1, 0], seqs=seqs, tm=tm, pos0=pos0)
    x = ffn(x, mod, row0 // ffn_seqs, 2, ffn_w_in, ffn_w_down, ln_g, ln_b, l, 1)
    return x, sconv[None], spool[None]


def kernel(x_prompt, x_sample, cache_conv, cache_pool, c_prompt, c_sample, ada_w, ada_b, ln_g, ln_b,
           ffn_w_in, ffn_w_down, mix_w_in, mix_b_in, conv_w, conv_b, conv_ln_g, conv_ln_b, pool_w,
           pool_b, pool_scale, mix_w_out, mix_b_out):
    nb_p, nb_s = x_prompt.shape[0], x_sample.shape[0]
    mod = _ada_mod(jnp.concatenate([c_sample, c_prompt], axis=0), ada_w[0], ada_b[0])
    mod = mod.reshape(nb_s + nb_p, N_MOD, 1, D_MODEL)
    ln_shape = ln_g.shape[:-1] + (1, D_MODEL)
    weights = (ln_g.reshape(ln_shape), ln_b.reshape(ln_shape), ffn_w_in.astype(BF16),
               ffn_w_down.astype(BF16), mix_w_in.astype(BF16), mix_b_in, conv_w, conv_b, conv_ln_g,
               conv_ln_b, pool_w.astype(BF16), pool_b, pool_scale, mix_w_out.astype(BF16), mix_b_out)
    zero_conv = jnp.zeros((DEPTH, nb_p, CONV_HIST, CONV_CH), F32)
    zero_pool = jnp.zeros((DEPTH, nb_p, POOL_HIST, POOL_CH), F32)
    t_s = x_sample.shape[1]
    y_p, sc_p, sp_p = _trunk(x_prompt, mod, nb_s, zero_conv, zero_pool, 0, weights,
                             seqs=1, tm=MIXER_TILE, ffn_seqs=1, ffn_tm=FFN_TILE)
    y_s, sc_s, sp_s = _trunk(x_sample, mod, 0, cache_conv, cache_pool, PAST_LEN, weights,
                             seqs=N_SEG, tm=t_s, ffn_seqs=FFN_TILE // t_s, ffn_tm=t_s)
    return (y_p, y_s, sc_p, sp_p, sc_s, sp_s)
```

```python
import functools

import jax
import jax.numpy as jnp
from jax.experimental import pallas as pl
from jax.experimental.pallas import tpu as pltpu

D_MODEL = 1024
D_FF = 2816
CONV_CH = 512
POOL_CH = 512
CONV_WIDTH = 31
CONV_HIST = CONV_WIDTH - 1
POOL_WINDOWS = (2, 4, 8, 16)
POOL_GROUP = POOL_CH // len(POOL_WINDOWS)
POOL_HIST = max(POOL_WINDOWS) - 1
IN_COLS = 2 * CONV_CH + POOL_CH
N_MOD = 9
DEPTH = 1
ALPHA = (2.0 * DEPTH) ** 0.25
LN_EPS = 1e-5
PAST_LEN = 4096

SUBLANES = 8
LANES = 128
N_SEG = SUBLANES
CONV_SLABS = CONV_CH // LANES
POOL_SLABS = POOL_CH // LANES
MIX_CHUNK_ROWS = 8 * SUBLANES
FFN_CHUNK = 256
FFN_TILE = 1024
FFN_GROUP_ROWS = 512
MIXER_TILE = 1024
VMEM_LIMIT_BYTES = 56 * 1024 * 1024

F32 = jnp.float32
BF16 = jnp.bfloat16


def _layernorm(y, g, b):
    mu = jnp.mean(y, axis=-1, keepdims=True)
    d = y - mu
    var = jnp.mean(d * d, axis=-1, keepdims=True)
    return d * jax.lax.rsqrt(var + LN_EPS) * g + b


def _layernorm_slabs(y, g, b):
    n = y.shape[0] * y.shape[2]
    mu = jnp.sum(jnp.sum(y, axis=0), axis=-1, keepdims=True) / n
    d = y - mu
    var = jnp.sum(jnp.sum(d * d, axis=0), axis=-1, keepdims=True) / n
    return d * jax.lax.rsqrt(var + LN_EPS) * g + b


def _sigmoid(x):
    return 1.0 / (1.0 + jnp.exp(-x))


def _const_spec(shape):
    nd = len(shape)
    return pl.BlockSpec(shape, lambda *_: (0,) * nd, pipeline_mode=pl.Buffered(1))


def _stack_spec(shape, *lead):
    nd = len(shape) - len(lead)
    return pl.BlockSpec((None,) * len(lead) + tuple(shape[len(lead):]),
                        lambda *_: tuple(lead) + (0,) * nd, pipeline_mode=pl.Buffered(1))


def _mod_specs(seqs, row_block0, k):
    return [pl.BlockSpec((seqs, None, 1, D_MODEL),
                         lambda i, j, m=m: (row_block0 + i, 3 * k + m, 0, 0)) for m in range(3)]


def _ada_kernel(c_ref, w_ref, b_ref, o_ref):
    c = c_ref[...]
    s = (c * _sigmoid(c)).astype(BF16)
    o_ref[...] = jnp.dot(s, w_ref[...].astype(BF16), preferred_element_type=F32) + b_ref[...]


def _ada_mod(c, w, b):
    rows = c.shape[0]
    n = w.shape[1]
    bn = D_MODEL
    return pl.pallas_call(
        _ada_kernel,
        grid=(n // bn,),
        in_specs=[
            pl.BlockSpec((rows, D_MODEL), lambda j: (0, 0)),
            pl.BlockSpec((D_MODEL, bn), lambda j: (0, j)),
            pl.BlockSpec((1, bn), lambda j: (0, j)),
        ],
        out_specs=pl.BlockSpec((rows, bn), lambda j: (0, j)),
        out_shape=jax.ShapeDtypeStruct((rows, n), F32),
        compiler_params=pltpu.CompilerParams(dimension_semantics=("arbitrary",)),
        name="ada_mod",
    )(c, w, b.reshape(1, n))


def _ffn_kernel(x_ref, sh_ref, sc_ref, gt_ref, win_ref, wdn_ref, lng_ref, lnb_ref, o_ref, *, fc):
    s, tm, _ = x_ref.shape
    n_groups = max(1, (s * tm) // FFN_GROUP_ROWS)
    gs, gt_rows = (s // n_groups, tm) if s >= n_groups else (s, tm // n_groups)
    for p in range(n_groups):
        if s >= n_groups:
            rows = (slice(p * gs, (p + 1) * gs), slice(None))
        else:
            rows = (slice(None), slice(p * gt_rows, (p + 1) * gt_rows))
        x = x_ref[rows]
        sc, sh, gt = sc_ref[rows[0]], sh_ref[rows[0]], gt_ref[rows[0]]
        u = (x * (1.0 + sc) + sh).reshape(gs * gt_rows, D_MODEL).astype(BF16)
        acc = jnp.zeros((gs * gt_rows, D_MODEL), F32)
        for j in range(D_FF // fc):
            g = jnp.dot(u, win_ref[:, j * fc:(j + 1) * fc], preferred_element_type=F32)
            v = jnp.dot(u, win_ref[:, D_FF + j * fc:D_FF + (j + 1) * fc], preferred_element_type=F32)
            h = (g * _sigmoid(g) * v).astype(BF16)
            acc = acc + jnp.dot(h, wdn_ref[j * fc:(j + 1) * fc, :], preferred_element_type=F32)
        y = ALPHA * x + (0.5 * gt) * acc.reshape(gs, gt_rows, D_MODEL)
        o_ref[rows] = _layernorm(y, lng_ref[...], lnb_ref[...])


def _ffn_step(x, mod, row_block0, k, w_in, w_down, ln_g, ln_b, layer, half, *, seqs, tm, fc):
    b, t, _ = x.shape
    xspec = pl.BlockSpec((seqs, tm, D_MODEL), lambda i, j: (i, j, 0))
    return pl.pallas_call(
        functools.partial(_ffn_kernel, fc=fc),
        grid=(b // seqs, t // tm),
        in_specs=[xspec] + _mod_specs(seqs, row_block0, k) + [
            _stack_spec(w_in.shape, layer, half), _stack_spec(w_down.shape, layer, half),
            _stack_spec(ln_g.shape, layer, k), _stack_spec(ln_b.shape, layer, k)],
        out_specs=xspec,
        out_shape=jax.ShapeDtypeStruct(x.shape, F32),
        compiler_params=pltpu.CompilerParams(
            dimension_semantics=("arbitrary", "arbitrary"), vmem_limit_bytes=VMEM_LIMIT_BYTES),
        name="ffn_step",
    )(x, mod, mod, mod, w_in, w_down, ln_g, ln_b)


def _seg_rows(start, count):
    return pl.ds(start, count, stride=N_SEG)


def _fill_history(buf, hist, seg_len, cache_ref, carry, state_ref, chained):
    tail = SUBLANES * seg_len
    for c in range(buf.shape[0]):
        lanes = slice(c * LANES, (c + 1) * LANES)
        if chained:
            buf[c, _seg_rows(0, hist), :] = carry[:, lanes]
            for s in range(1, N_SEG):
                buf[c, _seg_rows(s, hist), :] = buf[c, _seg_rows(tail + s - 1, hist), :]
            new = buf[c, _seg_rows(tail + N_SEG - 1, hist), :]
            carry[:, lanes] = new
            state_ref[0, :, lanes] = new
        else:
            for s in range(N_SEG):
                buf[c, _seg_rows(s, hist), :] = cache_ref[s, :, lanes]
                state_ref[s, :, lanes] = buf[c, _seg_rows(tail + s, hist), :]


def _mixer_kernel(x_ref, sh_ref, sc_ref, gt_ref, cc_ref, pc_ref, win_ref, bin_ref, cw_ref, cb_ref,
                  clg_ref, clb_ref, pw_ref, pb_ref, ps_ref, wout_ref, bout_ref, lng_ref, lnb_ref,
                  o_ref, sconv_ref, spool_ref, gbuf, pbuf, abuf, ccarry, pcarry, *, pos0, chained):
    n_seq, tt, _ = x_ref.shape
    rows = n_seq * tt
    seg_len = rows // N_SEG
    seg_span = SUBLANES * seg_len
    t_idx = pl.program_id(1)

    if chained:
        @pl.when(t_idx == 0)
        def _():
            ccarry[...] = cc_ref[0]
            pcarry[...] = pc_ref[0]

    x = x_ref[...]
    u = (x * (1.0 + sc_ref[...]) + sh_ref[...]).reshape(rows, D_MODEL).astype(BF16)
    z = jnp.dot(u, win_ref[...], preferred_element_type=F32) + bin_ref[...]
    glu = z[:, :CONV_CH] * _sigmoid(z[:, CONV_CH:2 * CONV_CH])
    zp = z[:, 2 * CONV_CH:]

    for s in range(N_SEG):
        seg = slice(s * seg_len, (s + 1) * seg_len)
        for c in range(CONV_SLABS):
            gbuf[c, _seg_rows(SUBLANES * CONV_HIST + s, seg_len), :] = glu[seg, c * LANES:(c + 1) * LANES]
        for c in range(POOL_SLABS):
            pbuf[c, _seg_rows(SUBLANES * POOL_HIST + s, seg_len), :] = zp[seg, c * LANES:(c + 1) * LANES]
    _fill_history(gbuf, CONV_HIST, seg_len, cc_ref, ccarry, sconv_ref, chained)
    _fill_history(pbuf, POOL_HIST, seg_len, pc_ref, pcarry, spool_ref, chained)

    r = jax.lax.broadcasted_iota(jnp.int32, (MIX_CHUNK_ROWS, 1), 0)
    if chained:
        pos_base = pos0 + t_idx * rows + (r % SUBLANES) * seg_len + r // SUBLANES
    else:
        pos_base = pos0 + t_idx * seg_len + r // SUBLANES
    for q in range(seg_span // MIX_CHUNK_ROWS):
        r0 = q * MIX_CHUNK_ROWS
        conv = cb_ref[...] + gbuf[:, r0:r0 + MIX_CHUNK_ROWS, :] * cw_ref[0]
        for k in range(1, CONV_WIDTH):
            lo = r0 + SUBLANES * k
            conv = conv + gbuf[:, lo:lo + MIX_CHUNK_ROWS, :] * cw_ref[k]
        a = _layernorm_slabs(conv, clg_ref[...], clb_ref[...])
        abuf[0:CONV_SLABS, r0:r0 + MIX_CHUNK_ROWS, :] = a * _sigmoid(a)
        pos = pos_base + r0 // SUBLANES
        for gi, w in enumerate(POOL_WINDOWS):
            lo = r0 + SUBLANES * POOL_HIST
            cur = pbuf[gi, lo:lo + MIX_CHUNK_ROWS, :]
            wsum = cur
            for i in range(1, w):
                wsum = wsum + pbuf[gi, lo - SUBLANES * i:lo - SUBLANES * i + MIX_CHUNK_ROWS, :]
            inv_cnt = 1.0 / jnp.minimum(pos + 1, w).astype(F32)
            abuf[CONV_SLABS + gi, r0:r0 + MIX_CHUNK_ROWS, :] = wsum * inv_cnt - cur

    for gi in range(len(POOL_WINDOWS)):
        pooled = abuf[CONV_SLABS + gi].astype(BF16)
        pm = jnp.dot(pooled, pw_ref[gi], preferred_element_type=F32) + pb_ref[gi]
        abuf[CONV_SLABS + gi] = pm * ps_ref[gi]

    mixed = jnp.concatenate(
        [jnp.concatenate([abuf[c, _seg_rows(s, seg_len), :] for c in range(abuf.shape[0])], axis=-1)
         for s in range(N_SEG)], axis=0).astype(BF16)
    m = jnp.dot(mixed, wout_ref[...], preferred_element_type=F32) + bout_ref[...]
    y = ALPHA * x + gt_ref[...] * m.reshape(n_seq, tt, D_MODEL)
    o_ref[...] = _layernorm(y, lng_ref[...], lnb_ref[...])


def _mixer_step(x, mod, row_block0, conv_cache, pool_cache, w_in, b_in, conv_w, conv_b, cln_g,
                cln_b, pool_w, pool_b, pool_scale, w_out, b_out, ln_g, ln_b, *, seqs, tm, pos0):
    b, t, _ = x.shape
    chained = seqs == 1
    seg_len = tm // N_SEG if chained else tm
    assert seqs in (1, N_SEG) and (chained or tm == t)
    assert seg_len >= CONV_HIST and seg_len % SUBLANES == 0
    xspec = pl.BlockSpec((seqs, tm, D_MODEL), lambda i, j: (i, j, 0))
    cspec = pl.BlockSpec((seqs, CONV_HIST, CONV_CH), lambda i, j: (i, 0, 0))
    pspec = pl.BlockSpec((seqs, POOL_HIST, POOL_CH), lambda i, j: (i, 0, 0))
    row = lambda v: v.reshape(1, v.shape[-1])
    slabs = lambda v: v.reshape(v.shape[:-1] + (v.shape[-1] // LANES, 1, LANES))
    consts = [w_in, row(b_in), slabs(conv_w), slabs(conv_b), slabs(cln_g), slabs(cln_b), pool_w,
              pool_b.reshape(len(POOL_WINDOWS), 1, POOL_GROUP), slabs(pool_scale), w_out, row(b_out),
              row(ln_g), row(ln_b)]
    return pl.pallas_call(
        functools.partial(_mixer_kernel, pos0=pos0, chained=chained),
        grid=(b // seqs, t // tm),
        in_specs=([xspec] + _mod_specs(seqs, row_block0, 1) + [cspec, pspec]
                  + [_const_spec(c.shape) for c in consts]),
        out_specs=[xspec, cspec, pspec],
        out_shape=[jax.ShapeDtypeStruct(x.shape, F32),
                   jax.ShapeDtypeStruct((b, CONV_HIST, CONV_CH), F32),
                   jax.ShapeDtypeStruct((b, POOL_HIST, POOL_CH), F32)],
        scratch_shapes=[pltpu.VMEM((CONV_SLABS, SUBLANES * (CONV_HIST + seg_len), LANES), F32),
                        pltpu.VMEM((POOL_SLABS, SUBLANES * (POOL_HIST + seg_len), LANES), F32),
                        pltpu.VMEM((CONV_SLABS + POOL_SLABS, SUBLANES * seg_len, LANES), F32),
                        pltpu.VMEM((CONV_HIST, CONV_CH), F32),
                        pltpu.VMEM((POOL_HIST, POOL_CH), F32)],
        compiler_params=pltpu.CompilerParams(
            dimension_semantics=("arbitrary", "arbitrary"), vmem_limit_bytes=VMEM_LIMIT_BYTES),
        name="mixer_step",
    )(x, mod, mod, mod, conv_cache, pool_cache, *consts)


def _trunk(x, mod, row0, conv_cache, pool_cache, pos0, weights, *, seqs, tm, ffn_seqs, ffn_tm):
    (ln_g, ln_b, ffn_w_in, ffn_w_down, mix_w_in, mix_b_in, conv_w, conv_b, conv_ln_g, conv_ln_b,
     pool_w, pool_b, pool_scale, mix_w_out, mix_b_out) = weights
    l = 0
    ffn = functools.partial(_ffn_step, seqs=ffn_seqs, tm=ffn_tm, fc=FFN_CHUNK)
    x = ffn(x, mod, row0 // ffn_seqs, 0, ffn_w_in, ffn_w_down, ln_g, ln_b, l, 0)
    x, sconv, spool = _mixer_step(
        x, mod, row0 // seqs, conv_cache[l], pool_cache[l], mix_w_in[l], mix_b_in[l],
        conv_w[l], conv_b[l], conv_ln_g[l], conv_ln_b[l], pool_w[l], pool_b[l], pool_scale[l],
        mix_w_out[l], mix_b_out[l], ln_g[l, 1, 0], ln_b[l, 1, 0], seqs=seqs, tm=tm, pos0=pos0)
    x = ffn(x, mod, row0 // ffn_seqs, 2, ffn_w_in, ffn_w_down, ln_g, ln_b, l, 1)
    return x, sconv[None], spool[None]


def kernel(x_prompt, x_sample, cache_conv, cache_pool, c_prompt, c_sample, ada_w, ada_b, ln_g, ln_b,
           ffn_w_in, ffn_w_down, mix_w_in, mix_b_in, conv_w, conv_b, conv_ln_g, conv_ln_b, pool_w,
           pool_b, pool_scale, mix_w_out, mix_b_out):
    nb_p, nb_s = x_prompt.shape[0], x_sample.shape[0]
    mod = _ada_mod(jnp.concatenate([c_sample, c_prompt], axis=0), ada_w[0], ada_b[0])
    mod = mod.reshape(nb_s + nb_p, N_MOD, 1, D_MODEL)
    ln_shape = ln_g.shape[:-1] + (1, D_MODEL)
    weights = (ln_g.reshape(ln_shape), ln_b.reshape(ln_shape), ffn_w_in.astype(BF16),
               ffn_w_down.astype(BF16), mix_w_in.astype(BF16), mix_b_in, conv_w, conv_b, conv_ln_g,
               conv_ln_b, pool_w.astype(BF16), pool_b, pool_scale, mix_w_out.astype(BF16), mix_b_out)
    zero_conv = jnp.zeros((DEPTH, nb_p, CONV_HIST, CONV_CH), F32)
    zero_pool = jnp.zeros((DEPTH, nb_p, POOL_HIST, POOL_CH), F32)
    t_s = x_sample.shape[1]
    y_p, sc_p, sp_p = _trunk(x_prompt, mod, nb_s, zero_conv, zero_pool, 0, weights,
                             seqs=1, tm=MIXER_TILE, ffn_seqs=1, ffn_tm=FFN_TILE)
    y_s, sc_s, sp_s = _trunk(x_sample, mod, 0, cache_conv, cache_pool, PAST_LEN, weights,
                             seqs=N_SEG, tm=t_s, ffn_seqs=FFN_TILE // t_s, ffn_tm=t_s)
    return (y_p, y_s, sc_p, sp_p, sc_s, sp_s)
```
